```python
import math
import jax
import jax.numpy as jnp
from jax import lax
import numpy as np

D_MODEL = 2048
BATCH = 8
SEQ = 4096
DEPTH = 2

GRID_W = 64
CTX_LEN = 256
EPS = 1e-6
F32 = jnp.float32

POOL_WIDTH = D_MODEL // 4
POOL_WINDOWS = (2, 4, 8, 16)
POOL_GROUPS = len(POOL_WINDOWS)
POOL_GROUP_W = POOL_WIDTH // POOL_GROUPS
S5_WIDTH = D_MODEL // 4
S5_GROUP_W = 16
S5_GROUPS = S5_WIDTH // S5_GROUP_W
S5_STATE = 64
S5_DT_MIN = 0.001
S5_DT_MAX = 0.1
ATTN_WIDTH = D_MODEL // 2
DIFF_HEADS = 8
DIFF_V_DIM = ATTN_WIDTH // DIFF_HEADS
DIFF_QK_DIM = DIFF_V_DIM // 2
ROPE_PAIRS = DIFF_QK_DIM // 4
ROPE_THETA = 10000.0
Q_BLOCK = 128
N_BRANCH = 3
IN_POOL = 0
IN_S5 = IN_POOL + POOL_WIDTH
IN_Q = IN_S5 + S5_WIDTH
IN_K = IN_Q + ATTN_WIDTH
IN_V = IN_K + ATTN_WIDTH
IN_GATE = IN_V + ATTN_WIDTH
IN_TOTAL = IN_GATE + N_BRANCH * D_MODEL
PEER_HEADS = 8
PEER_N_KEYS = 128
PEER_N_EXPERTS = PEER_N_KEYS * PEER_N_KEYS
PEER_TOPK = 16
PEER_QUERY_DIM = 256
PEER_HALF = PEER_QUERY_DIM // 2
PEER_BLOCK = 128

kernel_name = 'hybrid_pool_s5_diffattn_peer_dit'


def rmsnorm(x, g):
    xf = x.astype(F32)
    y = xf * lax.rsqrt(jnp.mean(xf * xf, axis=-1, keepdims=True) + EPS) * g.astype(F32)
    return y.astype(x.dtype)


def modulate(h, shift, scale):
    return h * (1.0 + scale) + shift


def pool_mix(p, w_pool, scale):
    b, s, _ = p.shape
    pf = p.astype(F32)
    cs = jnp.concatenate([jnp.zeros((b, 1, POOL_WIDTH), F32), jnp.cumsum(pf, axis=1)], axis=1)
    t = jnp.arange(s)
    parts = []
    for g, win in enumerate(POOL_WINDOWS):
        lo = jnp.clip(t - win // 2, 0, s - 1)
        hi = jnp.clip(t + win - 1 - win // 2, 0, s - 1)
        sl = slice(g * POOL_GROUP_W, (g + 1) * POOL_GROUP_W)
        csg = cs[..., sl]
        mean = (jnp.take(csg, hi + 1, axis=1) - jnp.take(csg, lo, axis=1)) / (hi - lo + 1).astype(F32)[:, None]
        parts.append(mean - pf[..., sl])
    d = jnp.stack(parts, axis=2)
    y = jnp.einsum('bsgi,gio->bsgo', d, w_pool.astype(F32)).reshape(b, s, POOL_WIDTH)
    return (y * scale.astype(F32)).astype(p.dtype)


def s5_discretise(a_re, a_im, log_dt, b_re, b_im):
    dt = jnp.exp(log_dt)[:, None]
    mag = jnp.exp(a_re * dt)
    lb_re, lb_im = mag * jnp.cos(a_im * dt), mag * jnp.sin(a_im * dt)
    den = a_re * a_re + a_im * a_im
    coef_re = ((lb_re - 1.0) * a_re + lb_im * a_im) / den
    coef_im = (lb_im * a_re - (lb_re - 1.0) * a_im) / den
    bb_re = coef_re[..., None] * b_re - coef_im[..., None] * b_im
    bb_im = coef_re[..., None] * b_im + coef_im[..., None] * b_re
    return lb_re, lb_im, bb_re, bb_im


def _ssm_combine(e1, e2):
    a1r, a1i, b1r, b1i = e1
    a2r, a2i, b2r, b2i = e2
    return (a1r * a2r - a1i * a2i, a1r * a2i + a1i * a2r,
            a2r * b1r - a2i * b1i + b2r, a2r * b1i + a2i * b1r + b2i)


def s5_states(u, lb_re, lb_im, bb_re, bb_im, reverse, h0_re=None, h0_im=None):
    s = u.shape[1]
    bu_re = jnp.einsum('bsgh,gph->bsgp', u, bb_re)
    bu_im = jnp.einsum('bsgh,gph->bsgp', u, bb_im)
    shape = (1, s) + lb_re.shape
    acc_re, acc_im, s_re, s_im = lax.associative_scan(
        _ssm_combine, (jnp.broadcast_to(lb_re, shape), jnp.broadcast_to(lb_im, shape), bu_re, bu_im),
        axis=1, reverse=reverse)
    if h0_re is not None:
        h0_re, h0_im = h0_re[:, None], h0_im[:, None]
        s_re, s_im = (s_re + acc_re * h0_re - acc_im * h0_im,
                      s_im + acc_re * h0_im + acc_im * h0_re)
    return s_re, s_im


def s5_readout(s_re, s_im, c_re, c_im):
    return jnp.einsum('ghp,bsgp->bsgh', c_re, s_re) - jnp.einsum('ghp,bsgp->bsgh', c_im, s_im)


def s5_glu(y, w_glu):
    g = jax.nn.gelu(y)
    return g * jax.nn.sigmoid(g @ w_glu.astype(F32))


def s5_branch(u_lat, u_ctx, a_re, a_im, log_dt, b_re, b_im, c_re, c_im, d_skip, w_glu, need_ctx):
    dtype = u_lat.dtype
    bsz, s, _ = u_lat.shape
    ul = u_lat.astype(F32).reshape(bsz, s, S5_GROUPS, S5_GROUP_W)
    uc = u_ctx.astype(F32).reshape(bsz, u_ctx.shape[1], S5_GROUPS, S5_GROUP_W)
    dsk = d_skip.astype(F32).reshape(S5_GROUPS, S5_GROUP_W)
    yl = ul * dsk
    yc = uc * dsk
    for dirn in range(2):
        rev = dirn == 1
        end = 0 if rev else -1
        lb_re, lb_im, bb_re, bb_im = s5_discretise(a_re[dirn].astype(F32), a_im[dirn].astype(F32),
                                                   log_dt[dirn].astype(F32), b_re[dirn].astype(F32),
                                                   b_im[dirn].astype(F32))
        cr, ci = c_re[dirn].astype(F32), c_im[dirn].astype(F32)
        sc_re, sc_im = s5_states(uc, lb_re, lb_im, bb_re, bb_im, rev)
        sl_re, sl_im = s5_states(ul, lb_re, lb_im, bb_re, bb_im, rev, sc_re[:, end], sc_im[:, end])
        yl = yl + s5_readout(sl_re, sl_im, cr, ci)
        if need_ctx:
            yc = yc + s5_readout(sc_re, sc_im, cr, ci)
    out_l = s5_glu(yl.reshape(bsz, s, S5_WIDTH), w_glu).astype(dtype)
    out_c = s5_glu(yc.reshape(bsz, u_ctx.shape[1], S5_WIDTH), w_glu).astype(dtype) if need_ctx else None
    return out_l, out_c


def rope1d(x, cos, sin):
    h = x.shape[-1] // 2
    x1, x2 = x[..., :h], x[..., h:]
    cos, sin = cos.astype(x.dtype), sin.astype(x.dtype)
    return jnp.concatenate([x1 * cos - x2 * sin, x1 * sin + x2 * cos], axis=-1)


def axial_rope(x, cos_r, sin_r, cos_c, sin_c):
    half = x.shape[-1] // 2
    return jnp.concatenate([rope1d(x[..., :half], cos_r, sin_r), rope1d(x[..., half:], cos_c, sin_c)], axis=-1)


def split_qk(z):
    b, s, _ = z.shape
    return z.reshape(b, s, DIFF_HEADS, 2, DIFF_QK_DIM).transpose(0, 2, 3, 1, 4)


def split_v(z):
    b, s, _ = z.shape
    return z.reshape(b, s, DIFF_HEADS, DIFF_V_DIM).transpose(0, 2, 1, 3)


def diff_core(q, k, v, lam):
    sc = jnp.einsum('bhcqd,bhckd->bhcqk', q, k, preferred_element_type=F32) * (DIFF_QK_DIM ** -0.5)
    p = jax.nn.softmax(sc, axis=-1)
    a = p[:, :, 0] - lam * p[:, :, 1]
    return jnp.einsum('bhqk,bhkd->bhqd', a.astype(v.dtype), v)


def blocked_diff_attention(q, k, v, lam):
    b, h, _, s, dk = q.shape
    nb = s // Q_BLOCK
    qb = jnp.moveaxis(q.reshape(b, h, 2, nb, Q_BLOCK, dk), 3, 0)
    o = lax.map(lambda qq: diff_core(qq, k, v, lam), qb)
    return jnp.moveaxis(o, 0, 2).reshape(b, h, s, DIFF_V_DIM)


def diff_post(o, g, lam_init):
    b, h, s, dv = o.shape
    o = rmsnorm(o.transpose(0, 2, 1, 3), g) * (1.0 - lam_init)
    return o.reshape(b, s, h * dv)


def merge(gate_pre, y_pool, y_s5, y_attn, wbp, wbs, wba, wo):
    gp, gs, ga = jnp.split(jax.nn.sigmoid(gate_pre), N_BRANCH, axis=-1)
    return (gp * (y_pool @ wbp) + gs * (y_s5 @ wbs) + ga * (y_attn @ wba)) @ wo


def peer_ffn(h, wq, keys, u_tab, v_tab):
    b, s, d = h.shape
    hb = h.reshape((b * s) // PEER_BLOCK, PEER_BLOCK, d)

    def block(ht):
        q = (ht @ wq).reshape(PEER_BLOCK, PEER_HEADS, 2, PEER_HALF)
        sc = jnp.einsum('thcd,hckd->thck', q, keys, preferred_element_type=F32)
        v1, i1 = lax.top_k(sc[:, :, 0], PEER_TOPK)
        v2, i2 = lax.top_k(sc[:, :, 1], PEER_TOPK)
        cand = (v1[..., :, None] + v2[..., None, :]).reshape(PEER_BLOCK, PEER_HEADS, PEER_TOPK * PEER_TOPK)
        top, ci = lax.top_k(cand, PEER_TOPK)
        e = (jnp.take_along_axis(i1, ci // PEER_TOPK, axis=-1) * PEER_N_KEYS
             + jnp.take_along_axis(i2, ci % PEER_TOPK, axis=-1)).reshape(PEER_BLOCK, -1)
        g = jax.nn.softmax(top, axis=-1).reshape(PEER_BLOCK, -1)
        act = jnp.einsum('tkd,td->tk', u_tab[e], ht, preferred_element_type=F32)
        w = (g * jax.nn.gelu(act)).astype(ht.dtype)
        return jnp.einsum('tk,tkd->td', w, v_tab[e])

    return lax.map(block, hb).reshape(b, s, d)


def setup_inputs(seed: int = 0) -> dict:
    key = jax.random.key(seed)
    ks = jax.random.split(key, 40)
    L, D = DEPTH, D_MODEL
    G, P, H = S5_GROUPS, S5_STATE, S5_GROUP_W

    def nrm(k, shape, scale):
        return jax.random.normal(k, shape, F32) * scale

    a_im0 = jnp.broadcast_to(jnp.pi * jnp.arange(P, dtype=F32), (L, 2, G, P))
    return {
        'x': nrm(ks[0], (BATCH, SEQ, D), 1.0),
        'c': nrm(ks[1], (BATCH, D), 1.0),
        'ctx': nrm(ks[2], (BATCH, CTX_LEN, D), 1.0),
        'c_ctx': nrm(ks[3], (D,), 1.0),
        'ada_w': nrm(ks[4], (L, D, 6 * D), 0.5 * D ** -0.5),
        'ada_b': nrm(ks[5], (L, 6 * D), 0.02),
        'norm1_g': 1.0 + nrm(ks[6], (L, D), 0.02),
        'norm2_g': 1.0 + nrm(ks[7], (L, D), 0.02),
        'w_in': nrm(ks[8], (L, D, IN_TOTAL), D ** -0.5),
        'pool_w': nrm(ks[9], (L, POOL_GROUPS, POOL_GROUP_W, POOL_GROUP_W), POOL_GROUP_W ** -0.5),
        'pool_scale': 1.0 + nrm(ks[10], (L, POOL_WIDTH), 0.02),
        's5_a_re': -0.5 + nrm(ks[11], (L, 2, G, P), 0.01),
        's5_a_im': a_im0 + nrm(ks[12], (L, 2, G, P), 0.01),
        's5_log_dt': jax.random.uniform(ks[13], (L, 2, G), F32, math.log(S5_DT_MIN), math.log(S5_DT_MAX)),
        's5_b_re': nrm(ks[14], (L, 2, G, P, H), (2 * H) ** -0.5),
        's5_b_im': nrm(ks[15], (L, 2, G, P, H), (2 * H) ** -0.5),
        's5_c_re': nrm(ks[16], (L, 2, G, H, P), (2 * P) ** -0.5),
        's5_c_im': nrm(ks[17], (L, 2, G, H, P), (2 * P) ** -0.5),
        's5_d': nrm(ks[18], (L, S5_WIDTH), 1.0),
        's5_glu_w': nrm(ks[19], (L, S5_WIDTH, S5_WIDTH), S5_WIDTH ** -0.5),
        'diff_lq1': nrm(ks[20], (L, DIFF_QK_DIM), 0.1),
        'diff_lk1': nrm(ks[21], (L, DIFF_QK_DIM), 0.1),
        'diff_lq2': nrm(ks[22], (L, DIFF_QK_DIM), 0.1),
        'diff_lk2': nrm(ks[23], (L, DIFF_QK_DIM), 0.1),
        'diff_subln_g': 1.0 + nrm(ks[24], (L, DIFF_V_DIM), 0.02),
        'w_branch_pool': nrm(ks[25], (L, POOL_WIDTH, D), POOL_WIDTH ** -0.5),
        'w_branch_s5': nrm(ks[26], (L, S5_WIDTH, D), S5_WIDTH ** -0.5),
        'w_branch_attn': nrm(ks[27], (L, ATTN_WIDTH, D), ATTN_WIDTH ** -0.5),
        'w_out': nrm(ks[28], (L, D, D), D ** -0.5),
        'peer_wq': nrm(ks[29], (L, D, PEER_HEADS * PEER_QUERY_DIM), D ** -0.5),
        'peer_keys': nrm(ks[30], (L, PEER_HEADS, 2, PEER_N_KEYS, PEER_HALF), PEER_HALF ** -0.5),
        'peer_u': nrm(ks[31], (L, PEER_N_EXPERTS, D), D ** -0.5),
        'peer_v': nrm(ks[32], (L, PEER_N_EXPERTS, D), 1.0),
        'final_g': 1.0 + nrm(ks[33], (D,), 0.02),
    }


def reference(x, c, ctx, c_ctx, ada_w, ada_b, norm1_g, norm2_g, w_in, pool_w, pool_scale,
              s5_a_re, s5_a_im, s5_log_dt, s5_b_re, s5_b_im, s5_c_re, s5_c_im, s5_d, s5_glu_w,
              diff_lq1, diff_lk1, diff_lq2, diff_lk2, diff_subln_g,
              w_branch_pool, w_branch_s5, w_branch_attn, w_out,
              peer_wq, peer_keys, peer_u, peer_v, final_g):
    b, s, d = x.shape
    rows = s // GRID_W
    row_pos = jnp.repeat(jnp.arange(rows, dtype=F32), GRID_W)
    col_pos = jnp.tile(jnp.arange(GRID_W, dtype=F32), rows)
    freqs = ROPE_THETA ** (-jnp.arange(ROPE_PAIRS, dtype=F32) / ROPE_PAIRS)
    ang_r = row_pos[:, None] * freqs
    ang_c = col_pos[:, None] * freqs
    rope = (jnp.cos(ang_r), jnp.sin(ang_r), jnp.cos(ang_c), jnp.sin(ang_c))

    silu_lat = jax.nn.silu(c)[:, None, :]
    silu_ctx = jax.nn.silu(c_ctx)[None, None, :]
    xl, xc = x, ctx
    for l in range(DEPTH):
        need_ctx = l < DEPTH - 1
        mod_l = jnp.split(silu_lat @ ada_w[l] + ada_b[l], 6, axis=-1)
        mod_c = jnp.split(silu_ctx @ ada_w[l] + ada_b[l], 6, axis=-1)
        w = w_in[l]
        hl = modulate(rmsnorm(xl, norm1_g[l]), mod_l[0], mod_l[1])
        hc = modulate(rmsnorm(xc, norm1_g[l]), mod_c[0], mod_c[1])

        lam_init = 0.8 - 0.6 * math.exp(-0.3 * l)
        lam = (jnp.exp(jnp.sum(diff_lq1[l] * diff_lk1[l]).astype(F32))
               - jnp.exp(jnp.sum(diff_lq2[l] * diff_lk2[l]).astype(F32)) + lam_init)
        q_l = axial_rope(split_qk(hl @ w[:, IN_Q:IN_K]), *rope)
        k_l = axial_rope(split_qk(hl @ w[:, IN_K:IN_V]), *rope)
        v_l = split_v(hl @ w[:, IN_V:IN_GATE])
        k_c = split_qk(hc @ w[:, IN_K:IN_V])
        v_c = split_v(hc @ w[:, IN_V:IN_GATE])
        k_all = jnp.concatenate([k_l, k_c], axis=3)
        v_all = jnp.concatenate([v_l, v_c], axis=2)
        a_l = diff_post(blocked_diff_attention(q_l, k_all, v_all, lam), diff_subln_g[l], lam_init)

        s_l, s_c = s5_branch(hl @ w[:, IN_S5:IN_Q], hc @ w[:, IN_S5:IN_Q], s5_a_re[l], s5_a_im[l],
                             s5_log_dt[l], s5_b_re[l], s5_b_im[l], s5_c_re[l], s5_c_im[l], s5_d[l],
                             s5_glu_w[l], need_ctx)

        p_l = pool_mix(hl @ w[:, IN_POOL:IN_S5], pool_w[l], pool_scale[l])

        y_l = merge(hl @ w[:, IN_GATE:IN_TOTAL], p_l, s_l, a_l,
                    w_branch_pool[l], w_branch_s5[l], w_branch_attn[l], w_out[l])
        xl = xl + mod_l[2] * y_l
        xl = xl + mod_l[5] * peer_ffn(modulate(rmsnorm(xl, norm2_g[l]), mod_l[3], mod_l[4]),
                                      peer_wq[l], peer_keys[l], peer_u[l], peer_v[l])

        if need_ctx:
            q_c = split_qk(hc @ w[:, IN_Q:IN_K])
            a_c = diff_post(diff_core(q_c, k_c, v_c, lam), diff_subln_g[l], lam_init)
            p_c = pool_mix(hc @ w[:, IN_POOL:IN_S5], pool_w[l], pool_scale[l])
            y_c = merge(hc @ w[:, IN_GATE:IN_TOTAL], p_c, s_c, a_c,
                        w_branch_pool[l], w_branch_s5[l], w_branch_attn[l], w_out[l])
            xc = xc + mod_c[2] * y_c
            xc = xc + mod_c[5] * peer_ffn(modulate(rmsnorm(xc, norm2_g[l]), mod_c[3], mod_c[4]),
                                          peer_wq[l], peer_keys[l], peer_u[l], peer_v[l])
    return rmsnorm(xl, final_g)
```

```python
import functools
import math

import jax
import jax.numpy as jnp
from jax import lax
from jax.experimental import pallas as pl
from jax.experimental.pallas import tpu as pltpu

F32 = jnp.float32
BF16 = jnp.bfloat16
EPS = 1e-6

GRID_W = 64
POOL_WINDOWS = (2, 4, 8, 16)
POOL_GROUP_W = 128
POOL_WIDTH = 512
S5_WIDTH = 512
S5_GROUP_W = 16
S5_GROUPS = 32
S5_STATE = 64
ATTN_WIDTH = 1024
DIFF_HEADS = 8
DIFF_V_DIM = 128
DIFF_QK_DIM = 64
ROPE_PAIRS = 16
ROPE_THETA = 10000.0
PEER_HEADS = 8
PEER_N_KEYS = 128
PEER_TOPK = 16
PEER_HALF = 128
COL_POOL = 0
COL_S5 = 4
COL_Q = 8
COL_K = 16
COL_V = 24
COL_GATE = 32
N_COLS = 80
MOD_ROWS = 16
S5_CHUNK = 16
NEG_INF = float("-inf")

LANES = 128
VMEM_LIMIT_BYTES = 56 * 1024 * 1024


def _cparams(*sem):
    return pltpu.CompilerParams(dimension_semantics=sem, vmem_limit_bytes=VMEM_LIMIT_BYTES)


def _dot(a, b):
    return jnp.dot(a, b, preferred_element_type=F32)


def _dot_nt(a, b):
    return lax.dot_general(a, b, (((1,), (1,)), ((), ())), preferred_element_type=F32)


def _ada_kernel(c_ref, w_ref, b_ref, o_ref):
    c = c_ref[...]
    s = c * jax.nn.sigmoid(c)
    o_ref[...] = _dot(s.astype(BF16), w_ref[...].astype(BF16)) + b_ref[...]


def _ada(cc, ada_w, ada_b):
    L, D, D6 = ada_w.shape
    tn = 1024
    return pl.pallas_call(
        _ada_kernel,
        grid=(L, D6 // tn),
        in_specs=[
            pl.BlockSpec((MOD_ROWS, D), lambda l, j: (0, 0)),
            pl.BlockSpec((None, D, tn), lambda l, j: (l, 0, j)),
            pl.BlockSpec((None, 1, tn), lambda l, j: (l, 0, j)),
        ],
        out_specs=pl.BlockSpec((None, MOD_ROWS, tn), lambda l, j: (l, 0, j)),
        out_shape=jax.ShapeDtypeStruct((L, MOD_ROWS, D6), F32),
        compiler_params=_cparams("parallel", "parallel"),
        name="ada",
    )(cc, ada_w, ada_b.reshape(L, 1, D6))


def _mod_spec(D, mod_base, which, tiles_per_row):
    if tiles_per_row is None:
        return pl.BlockSpec((None, 1, D), lambda i, *_: (mod_base + which, 0, 0))
    return pl.BlockSpec((None, 1, D), lambda i, *_: (mod_base + (i // tiles_per_row) * 6 + which, 0, 0))


def _norm_mod(x, g, shift, scale):
    ms = jnp.mean(x * x, axis=-1, keepdims=True)
    return (x * lax.rsqrt(ms + EPS) * g) * (1.0 + scale) + shift


def _inproj_kernel(x_ref, g_ref, sh_ref, sc_ref, w_ref, o_ref, h_scr):
    @pl.when(pl.program_id(1) == 0)
    def _():
        h_scr[...] = _norm_mod(x_ref[...], g_ref[...], sh_ref[...], sc_ref[...]).astype(BF16)

    o_ref[...] = _dot(h_scr[...], w_ref[...]).astype(o_ref.dtype)


def _inproj(x2, g, mods3, mod_base, tiles_per_row_fn, w_bf, layer, col_lo, col_hi, tm, tn):
    N, D = x2.shape
    ncol = (col_hi - col_lo) * LANES
    assert N % tm == 0 and ncol % tn == 0 and (col_lo * LANES) % tn == 0
    j0 = col_lo * LANES // tn
    tpr = tiles_per_row_fn(tm)
    return pl.pallas_call(
        _inproj_kernel,
        grid=(N // tm, ncol // tn),
        in_specs=[
            pl.BlockSpec((tm, D), lambda i, j: (i, 0)),
            pl.BlockSpec((None, 1, D), lambda i, j: (layer, 0, 0)),
            _mod_spec(D, mod_base, 0, tpr),
            _mod_spec(D, mod_base, 1, tpr),
            pl.BlockSpec((None, D, tn), lambda i, j: (layer, 0, j0 + j)),
        ],
        out_specs=pl.BlockSpec((tm, tn), lambda i, j: (i, j)),
        out_shape=jax.ShapeDtypeStruct((N, ncol), BF16),
        scratch_shapes=[pltpu.VMEM((tm, D), BF16)],
        compiler_params=_cparams("parallel", "arbitrary"),
        name="inproj",
    )(x2, g, mods3, mods3, w_bf)


def _rope(x, cos, sin_hi, sin_lo):
    return x * cos + pltpu.roll(x, 16, 1) * sin_hi + pltpu.roll(x, LANES - 16, 1) * sin_lo


def _attn_kernel(*refs, use_rope, n_lat, n_ctx, tk, lam_init):
    it = iter(refs)
    q_ref, kl_ref, vl_ref = next(it), next(it), next(it)
    if n_ctx:
        kc_ref, vc_ref = next(it), next(it)
    if use_rope:
        cosk_ref, shk_ref, slk_ref, cosq_ref, shq_ref, slq_ref = (next(it) for _ in range(6))
    lq1_ref, lk1_ref, lq2_ref, lk2_ref, g_ref = (next(it) for _ in range(5))
    o_ref = next(it)
    if use_rope:
        k_scr = next(it)

    tq = q_ref.shape[0]
    n_chunks = n_lat // tk

    if use_rope:
        @pl.when(pl.program_id(2) == 0)
        def _():
            for cidx in range(n_chunks):
                sl = pl.ds(cidx * tk, tk)
                k_scr[sl, :] = _rope(kl_ref[sl, :].astype(F32), cosk_ref[sl, :], shk_ref[sl, :],
                                     slk_ref[sl, :]).astype(BF16)
        k_src = k_scr
    else:
        k_src = kl_ref

    lam = (jnp.exp(jnp.sum(lq1_ref[...] * lk1_ref[...], axis=-1, keepdims=True))
           - jnp.exp(jnp.sum(lq2_ref[...] * lk2_ref[...], axis=-1, keepdims=True)) + lam_init)

    q = q_ref[...].astype(F32)
    if use_rope:
        q = _rope(q, cosq_ref[...], shq_ref[...], slq_ref[...])
    q = q * (DIFF_QK_DIM ** -0.5)
    lane = lax.broadcasted_iota(jnp.int32, q.shape, 1)
    qs = (jnp.where(lane < DIFF_QK_DIM, q, 0.0).astype(BF16),
          jnp.where(lane >= DIFF_QK_DIM, q, 0.0).astype(BF16))

    def update(state, kblk, vblk):
        new = []
        for c in range(2):
            m, l, acc = state[c]
            s = _dot_nt(qs[c], kblk)
            m_new = jnp.maximum(m, jnp.max(s, axis=-1, keepdims=True))
            alpha = jnp.exp(m - m_new)
            p = jnp.exp(s - m_new)
            l_new = alpha * l + jnp.sum(p, axis=-1, keepdims=True)
            acc_new = alpha * acc + _dot(p.astype(BF16), vblk)
            new.append((m_new, l_new, acc_new))
        return tuple(new)

    init = tuple((jnp.full((tq, 1), NEG_INF, F32), jnp.zeros((tq, 1), F32), jnp.zeros((tq, DIFF_V_DIM), F32))
                 for _ in range(2))

    def body(cidx, state):
        sl = pl.ds(pl.multiple_of(cidx * tk, tk), tk)
        return update(state, k_src[sl, :], vl_ref[sl, :])

    state = lax.fori_loop(0, n_chunks, body, init)
    if n_ctx:
        state = update(state, kc_ref[...], vc_ref[...])
    (_, l0, acc0), (_, l1, acc1) = state
    o = acc0 / l0 - lam * (acc1 / l1)
    o = o * lax.rsqrt(jnp.mean(o * o, axis=-1, keepdims=True) + EPS) * g_ref[...]
    o_ref[...] = (o * (1.0 - lam_init)).astype(o_ref.dtype)


def _attention(zq3, q_col, zk3, k_col, v_col, zc3, kc_col, vc_col, rope_tabs, lparams, layer, lam_init, tq, tk):
    B, Sq, _ = zq3.shape
    Sk = zk3.shape[1]
    use_rope = rope_tabs is not None
    n_ctx = 0 if zc3 is None else zc3.shape[1]
    assert Sq % tq == 0 and Sk % tk == 0
    H = DIFF_HEADS

    in_specs = [
        pl.BlockSpec((None, tq, LANES), lambda b, h, i: (b, i, q_col + h)),
        pl.BlockSpec((None, Sk, LANES), lambda b, h, i: (b, 0, k_col + h)),
        pl.BlockSpec((None, Sk, LANES), lambda b, h, i: (b, 0, v_col + h)),
    ]
    args = [zq3, zk3, zk3]
    if n_ctx:
        in_specs += [pl.BlockSpec((None, n_ctx, LANES), lambda b, h, i: (b, 0, kc_col + h)),
                     pl.BlockSpec((None, n_ctx, LANES), lambda b, h, i: (b, 0, vc_col + h))]
        args += [zc3, zc3]
    if use_rope:
        in_specs += [pl.BlockSpec((Sk, LANES), lambda b, h, i: (0, 0))] * 3
        in_specs += [pl.BlockSpec((tq, LANES), lambda b, h, i: (i, 0))] * 3
        args += list(rope_tabs) + list(rope_tabs)
    in_specs += [pl.BlockSpec((None, 1, DIFF_QK_DIM), lambda b, h, i: (layer, 0, 0))] * 4
    in_specs += [pl.BlockSpec((None, 1, DIFF_V_DIM), lambda b, h, i: (layer, 0, 0))]
    args += list(lparams)
    scratch = [pltpu.VMEM((Sk, LANES), BF16)] if use_rope else []
    return pl.pallas_call(
        functools.partial(_attn_kernel, use_rope=use_rope, n_lat=Sk, n_ctx=n_ctx, tk=tk, lam_init=lam_init),
        grid=(B, H, Sq // tq),
        in_specs=in_specs,
        out_specs=pl.BlockSpec((None, tq, LANES), lambda b, h, i: (b, i, h)),
        out_shape=jax.ShapeDtypeStruct((B, Sq, ATTN_WIDTH), BF16),
        scratch_shapes=scratch,
        compiler_params=_cparams("parallel", "parallel", "arbitrary"),
        name="attn",
    )(*args)


def _s5_weights(a_re, a_im, log_dt, b_re, b_im, c_re, c_im):
    T = S5_CHUNK
    hp = lax.Precision.HIGHEST
    ks = jnp.arange(T + 1, dtype=F32)
    mt, wc, vc, la, lb = [], [], [], [], []
    per_dir = []
    for d in range(2):
        dt = jnp.exp(log_dt[d])[:, None]
        are, aim = a_re[d], a_im[d]
        mag = jnp.exp(are * dt)
        lb_re, lb_im = mag * jnp.cos(aim * dt), mag * jnp.sin(aim * dt)
        den = are * are + aim * aim
        coef_re = ((lb_re - 1.0) * are + lb_im * aim) / den
        coef_im = (lb_im * are - (lb_re - 1.0) * aim) / den
        bb_re = coef_re[..., None] * b_re[d] - coef_im[..., None] * b_im[d]
        bb_im = coef_re[..., None] * b_im[d] + coef_im[..., None] * b_re[d]
        pmag = jnp.exp(ks[:, None, None] * (are * dt)[None])
        pang = ks[:, None, None] * (aim * dt)[None]
        p_re, p_im = pmag * jnp.cos(pang), pmag * jnp.sin(pang)
        cl_re = c_re[d][None] * p_re[:, :, None, :] - c_im[d][None] * p_im[:, :, None, :]
        cl_im = c_re[d][None] * p_im[:, :, None, :] + c_im[d][None] * p_re[:, :, None, :]
        kk = (jnp.einsum('kgip,gph->kgih', cl_re, bb_re, precision=hp)
              - jnp.einsum('kgip,gph->kgih', cl_im, bb_im, precision=hp))
        lbb_re = p_re[..., None] * bb_re[None] - p_im[..., None] * bb_im[None]
        lbb_im = p_re[..., None] * bb_im[None] + p_im[..., None] * bb_re[None]
        per_dir.append((kk, lbb_re, lbb_im, cl_re, cl_im, p_re[T], p_im[T]))

    G = a_re.shape[1]
    P = a_re.shape[2]
    Hc = S5_GROUP_W
    j = jnp.arange(T)[:, None]
    t = jnp.arange(T)[None, :]
    kf, kb = per_dir[0][0], per_dir[1][0]
    mf = jnp.where((t >= j)[:, :, None, None, None], kf[jnp.clip(t - j, 0, T)], 0.0)
    mb = jnp.where((j >= t)[:, :, None, None, None], kb[jnp.clip(j - t, 0, T)], 0.0)
    mtot = jnp.transpose(mf + mb, (2, 0, 4, 1, 3)).reshape(G, T * Hc, T * Hc)

    def wpart(lbb, order):
        return jnp.transpose(lbb[order], (1, 0, 3, 2)).reshape(G, T * Hc, P)

    f_ord = (T - 1) - jnp.arange(T)
    b_ord = jnp.arange(T)
    wcat = jnp.concatenate([wpart(per_dir[0][1], f_ord), wpart(per_dir[1][1], b_ord),
                            wpart(per_dir[0][2], f_ord), wpart(per_dir[1][2], b_ord)], axis=-1)

    def vpart(cl, order, sign):
        return sign * jnp.transpose(cl[order], (1, 3, 0, 2)).reshape(G, P, T * Hc)

    fv_ord = jnp.arange(T) + 1
    bv_ord = T - jnp.arange(T)
    vcat = jnp.concatenate([vpart(per_dir[0][3], fv_ord, 1.0), vpart(per_dir[1][3], bv_ord, 1.0),
                            vpart(per_dir[0][4], fv_ord, -1.0), vpart(per_dir[1][4], bv_ord, -1.0)], axis=1)
    la = jnp.concatenate([per_dir[0][5], per_dir[1][5]], axis=-1)[:, None, :]
    lb = jnp.concatenate([per_dir[0][6], per_dir[1][6]], axis=-1)[:, None, :]
    return mtot.astype(BF16), wcat.astype(BF16), vcat.astype(BF16), la, lb


def _s5_kernel(u_ref, mtot_ref, wcat_ref, vcat_ref, la_ref, lb_ref, y_ref, sc_scr, hs_scr, *, rows, n_lat, n_all):
    P = S5_STATE
    u = u_ref[...]
    sc_scr[...] = _dot(u, wcat_ref[...])
    la = la_ref[...]
    lb = lb_ref[...]
    fwd = lax.broadcasted_iota(jnp.int32, (rows, 2 * P), 1) < P

    def body(k, carry):
        a, b = carry
        kf = k + n_lat
        kf = jnp.where(kf >= n_all, kf - n_all, kf)
        rf = pl.ds(pl.multiple_of(kf * rows, rows), rows)
        rb = pl.ds(pl.multiple_of((n_all - 1 - k) * rows, rows), rows)
        hs_scr[rf, 0:P] = a[:, 0:P]
        hs_scr[rb, P:2 * P] = a[:, P:2 * P]
        hs_scr[rf, 2 * P:3 * P] = b[:, 0:P]
        hs_scr[rb, 3 * P:4 * P] = b[:, P:2 * P]
        sa = jnp.where(fwd, sc_scr[rf, 0:2 * P], sc_scr[rb, 0:2 * P])
        sb = jnp.where(fwd, sc_scr[rf, 2 * P:4 * P], sc_scr[rb, 2 * P:4 * P])
        return la * a - lb * b + sa, la * b + lb * a + sb

    zero = jnp.zeros((rows, 2 * P), F32)
    lax.fori_loop(0, n_all, body, (zero, zero))
    y_ref[...] = _dot(u, mtot_ref[...]) + _dot(hs_scr[...].astype(BF16), vcat_ref[...])


def _s5_mix(ug, weights, rows, n_lat, n_all):
    mtot, wcat, vcat, la, lb = weights
    G, R, W = ug.shape
    mat = pl.BlockSpec((None, W, W), lambda g: (g, 0, 0))
    vec = pl.BlockSpec((None, 1, 2 * S5_STATE), lambda g: (g, 0, 0))
    return pl.pallas_call(
        functools.partial(_s5_kernel, rows=rows, n_lat=n_lat, n_all=n_all),
        grid=(G,),
        in_specs=[pl.BlockSpec((None, R, W), lambda g: (g, 0, 0)), mat, mat, mat, vec, vec],
        out_specs=pl.BlockSpec((None, R, W), lambda g: (g, 0, 0)),
        out_shape=jax.ShapeDtypeStruct((G, R, W), F32),
        scratch_shapes=[pltpu.VMEM((R, W), F32), pltpu.VMEM((R, W), F32)],
        compiler_params=_cparams("parallel"),
        name="s5",
    )(ug, mtot, wcat, vcat, la, lb)


def _to_chunks(u3):
    B, S, _ = u3.shape
    u = u3.reshape(B, S // S5_CHUNK, S5_CHUNK, S5_GROUPS, S5_GROUP_W)
    return jnp.transpose(u, (3, 1, 0, 2, 4)).reshape(S5_GROUPS, (S // S5_CHUNK) * B, S5_CHUNK * S5_GROUP_W)


def _from_chunks(y, B, S):
    y = y.reshape(S5_GROUPS, S // S5_CHUNK, B, S5_CHUNK, S5_GROUP_W)
    return jnp.transpose(y, (2, 1, 3, 0, 4)).reshape(B, S, S5_WIDTH)


POOL_PAD = 16


def _pool_kernel(p_ref, w_ref, scale_ref, o_ref, pad_scr, *, seq, chunk):
    pad_scr[0:POOL_PAD, :] = jnp.zeros((POOL_PAD, POOL_WIDTH), F32)
    pad_scr[POOL_PAD + seq:POOL_PAD + seq + POOL_PAD, :] = jnp.zeros((POOL_PAD, POOL_WIDTH), F32)
    pad_scr[POOL_PAD:POOL_PAD + seq, :] = p_ref[...].astype(F32)
    for r0 in range(0, seq, chunk):
        t = r0 + lax.broadcasted_iota(jnp.int32, (chunk, 1), 0)
        for g, win in enumerate(POOL_WINDOWS):
            cols = slice(g * POOL_GROUP_W, (g + 1) * POOL_GROUP_W)
            lo_off, hi_off = -(win // 2), win - 1 - win // 2
            acc = None
            for k in range(lo_off, hi_off + 1):
                v = pad_scr[POOL_PAD + r0 + k:POOL_PAD + r0 + k + chunk, cols]
                acc = v if acc is None else acc + v
            lo = jnp.clip(t + lo_off, 0, seq - 1)
            hi = jnp.clip(t + hi_off, 0, seq - 1)
            cnt = (hi - lo + 1).astype(F32)
            d = acc / cnt - pad_scr[POOL_PAD + r0:POOL_PAD + r0 + chunk, cols]
            y = _dot(d.astype(BF16), w_ref[g]) * scale_ref[:, cols]
            o_ref[r0:r0 + chunk, cols] = y.astype(o_ref.dtype)


def _pool(z3, w_bf, scale, layer):
    B, S, _ = z3.shape
    chunk = min(512, S)
    return pl.pallas_call(
        functools.partial(_pool_kernel, seq=S, chunk=chunk),
        grid=(B,),
        in_specs=[
            pl.BlockSpec((None, S, POOL_WIDTH), lambda b: (b, 0, COL_POOL)),
            pl.BlockSpec((None, len(POOL_WINDOWS), POOL_GROUP_W, POOL_GROUP_W), lambda b: (layer, 0, 0, 0)),
            pl.BlockSpec((None, 1, POOL_WIDTH), lambda b: (layer, 0, 0)),
        ],
        out_specs=pl.BlockSpec((None, S, POOL_WIDTH), lambda b: (b, 0, 0)),
        out_shape=jax.ShapeDtypeStruct((B, S, POOL_WIDTH), BF16),
        scratch_shapes=[pltpu.VMEM((S + 2 * POOL_PAD, POOL_WIDTH), F32)],
        compiler_params=_cparams("parallel"),
        name="pool",
    )(z3, w_bf, scale)


def _merge_kernel(p_ref, ys_ref, u_ref, a_ref, gp_ref, gs_ref, ga_ref, dsk_ref, wglu_ref, wbp_ref, wbs_ref,
                  wba_ref, m_ref):
    y = u_ref[...].astype(F32) * dsk_ref[...] + ys_ref[...]
    g = jax.nn.gelu(y)
    s5 = g * jax.nn.sigmoid(_dot(g.astype(BF16), wglu_ref[...]))
    m = jax.nn.sigmoid(gp_ref[...].astype(F32)) * _dot(p_ref[...], wbp_ref[...])
    m = m + jax.nn.sigmoid(gs_ref[...].astype(F32)) * _dot(s5.astype(BF16), wbs_ref[...])
    m = m + jax.nn.sigmoid(ga_ref[...].astype(F32)) * _dot(a_ref[...], wba_ref[...])
    m_ref[...] = m.astype(m_ref.dtype)


def _merge(p2, ys2, z2, s5_col, gate_col, a2, dsk, wglu, wbp, wbs, wba, layer, tm):
    N = p2.shape[0]
    D = wbp.shape[-1]
    dc = D // LANES

    def wspec(k, n):
        return pl.BlockSpec((None, k, n), lambda i: (layer, 0, 0))

    return pl.pallas_call(
        _merge_kernel,
        grid=(N // tm,),
        in_specs=[
            pl.BlockSpec((tm, POOL_WIDTH), lambda i: (i, 0)),
            pl.BlockSpec((tm, S5_WIDTH), lambda i: (i, 0)),
            pl.BlockSpec((tm, S5_WIDTH), lambda i: (i, s5_col // 4)),
            pl.BlockSpec((tm, ATTN_WIDTH), lambda i: (i, 0)),
            pl.BlockSpec((tm, D), lambda i: (i, gate_col // dc)),
            pl.BlockSpec((tm, D), lambda i: (i, gate_col // dc + 1)),
            pl.BlockSpec((tm, D), lambda i: (i, gate_col // dc + 2)),
            pl.BlockSpec((None, 1, S5_WIDTH), lambda i: (layer, 0, 0)),
            wspec(S5_WIDTH, S5_WIDTH), wspec(POOL_WIDTH, D), wspec(S5_WIDTH, D), wspec(ATTN_WIDTH, D),
        ],
        out_specs=pl.BlockSpec((tm, D), lambda i: (i, 0)),
        out_shape=jax.ShapeDtypeStruct((N, D), BF16),
        compiler_params=_cparams("parallel"),
        name="merge",
    )(p2, ys2, z2, a2, z2, z2, z2, dsk, wglu, wbp, wbs, wba)


def _resid_mm_kernel(m_ref, w_ref, x_ref, g_ref, o_ref):
    o_ref[...] = x_ref[...] + g_ref[...] * _dot(m_ref[...], w_ref[...])


def _resid_mm(m2, w_bf, x2, mods3, mod_base, which, tiles_per_row_fn, layer, tm):
    N, D = x2.shape
    K = m2.shape[1]
    tpr = tiles_per_row_fn(tm)
    return pl.pallas_call(
        _resid_mm_kernel,
        grid=(N // tm,),
        in_specs=[
            pl.BlockSpec((tm, K), lambda i: (i, 0)),
            pl.BlockSpec((None, K, D), lambda i: (layer, 0, 0)),
            pl.BlockSpec((tm, D), lambda i: (i, 0)),
            _mod_spec(D, mod_base, which, tpr),
        ],
        out_specs=pl.BlockSpec((tm, D), lambda i: (i, 0)),
        out_shape=jax.ShapeDtypeStruct((N, D), F32),
        compiler_params=_cparams("parallel"),
        name="resid_mm",
    )(m2, w_bf, x2, mods3)


def _peer_prep_kernel(x_ref, g_ref, sh_ref, sc_ref, wq_ref, keys_ref, h_ref, st_ref):
    @pl.when(pl.program_id(1) == 0)
    def _():
        h_ref[...] = _norm_mod(x_ref[...], g_ref[...], sh_ref[...], sc_ref[...]).astype(BF16)

    q = _dot(h_ref[...], wq_ref[...]).astype(BF16)
    for c in range(2):
        qc = q[:, c * PEER_HALF:(c + 1) * PEER_HALF]
        kc = keys_ref[c * PEER_N_KEYS:(c + 1) * PEER_N_KEYS, :]
        st_ref[c * PEER_N_KEYS:(c + 1) * PEER_N_KEYS, :] = _dot_nt(kc, qc)


def _peer_prep(x2, g, mods3, mod_base, tiles_per_row_fn, wq_bf, keys_bf, layer, tm):
    N, D = x2.shape
    tpr = tiles_per_row_fn(tm)
    hw = 2 * PEER_HALF
    return pl.pallas_call(
        _peer_prep_kernel,
        grid=(N // tm, PEER_HEADS),
        in_specs=[
            pl.BlockSpec((tm, D), lambda i, j: (i, 0)),
            pl.BlockSpec((None, 1, D), lambda i, j: (layer, 0, 0)),
            _mod_spec(D, mod_base, 3, tpr),
            _mod_spec(D, mod_base, 4, tpr),
            pl.BlockSpec((None, D, hw), lambda i, j: (layer, 0, j)),
            pl.BlockSpec((None, None, 2 * PEER_N_KEYS, PEER_HALF), lambda i, j: (layer, j, 0, 0)),
        ],
        out_specs=[pl.BlockSpec((tm, D), lambda i, j: (i, 0)),
                   pl.BlockSpec((2 * PEER_N_KEYS, tm), lambda i, j: (j, i))],
        out_shape=[jax.ShapeDtypeStruct((N, D), BF16),
                   jax.ShapeDtypeStruct((PEER_HEADS * 2 * PEER_N_KEYS, N), F32)],
        compiler_params=_cparams("parallel", "arbitrary"),
        name="peer_prep",
    )(x2, g, mods3, mods3, wq_bf, keys_bf)


def _extract_top(cur, n):
    rows = cur.shape[0]
    ridx = lax.broadcasted_iota(jnp.int32, cur.shape, 0).astype(F32)
    out = []
    for _ in range(n):
        m = jnp.max(cur, axis=0, keepdims=True)
        out.append(m)
        first = jnp.min(jnp.where(cur == m, ridx, float(rows)), axis=0, keepdims=True)
        cur = jnp.where(ridx == first, NEG_INF, cur)
    return out


_CAND_PAIRS = tuple((j, k) for j in range(PEER_TOPK + 1) for k in range(PEER_TOPK + 1)
                    if (j + 1) * (k + 1) <= PEER_TOPK + 1)


_N_CAND = -(-len(_CAND_PAIRS) // 8) * 8


def _peer_route_kernel(st_ref, thr_ref, lk_ref, cand_scr):
    tt = st_ref.shape[-1]
    cand_scr[...] = jnp.full((_N_CAND, tt), NEG_INF, F32)
    for h in range(PEER_HEADS):
        s1 = st_ref[h, 0]
        s2 = st_ref[h, 1]
        v1 = _extract_top(s1, PEER_TOPK + 1)
        v2 = _extract_top(s2, PEER_TOPK + 1)
        for r, (j, k) in enumerate(_CAND_PAIRS):
            cand_scr[r:r + 1, :] = v1[j] + v2[k]
        cand = cand_scr[...]
        top = _extract_top(cand, PEER_TOPK + 1)
        tau = 0.5 * (top[PEER_TOPK - 1] + top[PEER_TOPK])
        mx = top[0]
        z = jnp.sum(jnp.where(cand > tau, jnp.exp(cand - mx), 0.0), axis=0, keepdims=True)
        thr_ref[h] = tau - s1
        lk_ref[h:h + 1, :] = tau - mx - jnp.log(z)


def _peer_route(st4, tt):
    H, _, nk, N = st4.shape
    return pl.pallas_call(
        _peer_route_kernel,
        grid=(N // tt,),
        in_specs=[pl.BlockSpec((H, 2, nk, tt), lambda i: (0, 0, 0, i))],
        out_specs=[pl.BlockSpec((H, nk, tt), lambda i: (0, 0, i)),
                   pl.BlockSpec((H, tt), lambda i: (0, i))],
        out_shape=[jax.ShapeDtypeStruct((H, nk, N), F32), jax.ShapeDtypeStruct((H, N), F32)],
        scratch_shapes=[pltpu.VMEM((_N_CAND, tt), F32)],
        compiler_params=_cparams("parallel"),
        name="peer_route",
    )(st4)


def _peer_dense_kernel(h_ref, u_ref, vt_ref, s2_ref, thr_ref, lk_ref, o_ref, acc, w_scr, *, n_sub):
    e = pl.program_id(1)

    @pl.when(e == 0)
    def _():
        acc[...] = jnp.zeros_like(acc)

    act = _dot_nt(u_ref[...], h_ref[...])
    gl = jax.nn.gelu(act)
    nk = PEER_N_KEYS
    for i in range(n_sub):
        i1 = e * n_sub + i
        gate = None
        for h in range(PEER_HEADS):
            d = s2_ref[h] - thr_ref[h, pl.ds(i1, 1), :]
            term = jnp.exp(jnp.where(d > 0.0, d + lk_ref[h:h + 1, :], NEG_INF))
            gate = term if gate is None else gate + term
        w_scr[i * nk:(i + 1) * nk, :] = (gate * gl[i * nk:(i + 1) * nk, :]).astype(BF16)
    acc[...] += _dot(vt_ref[...], w_scr[...])

    @pl.when(e == pl.num_programs(1) - 1)
    def _():
        o_ref[...] = acc[...].T.astype(o_ref.dtype)


def _peer_dense(h2, u_bf, vt_bf, st4, thr, lk, layer, tn, te):
    N, D = h2.shape
    H, _, nk, _ = st4.shape
    E = u_bf.shape[1]
    return pl.pallas_call(
        functools.partial(_peer_dense_kernel, n_sub=te // nk),
        grid=(N // tn, E // te),
        in_specs=[
            pl.BlockSpec((tn, D), lambda i, e: (i, 0)),
            pl.BlockSpec((None, te, D), lambda i, e: (layer, e, 0)),
            pl.BlockSpec((None, D, te), lambda i, e: (layer, 0, e)),
            pl.BlockSpec((H, None, nk, tn), lambda i, e: (0, 1, 0, i)),
            pl.BlockSpec((H, nk, tn), lambda i, e: (0, 0, i)),
            pl.BlockSpec((H, tn), lambda i, e: (0, i)),
        ],
        out_specs=pl.BlockSpec((tn, D), lambda i, e: (i, 0)),
        out_shape=jax.ShapeDtypeStruct((N, D), BF16),
        scratch_shapes=[pltpu.VMEM((D, tn), F32), pltpu.VMEM((te, tn), BF16)],
        compiler_params=_cparams("parallel", "arbitrary"),
        name="peer_dense",
    )(h2, u_bf, vt_bf, st4, thr, lk)


def _resid_kernel(x_ref, y_ref, g_ref, *rest, final):
    x = x_ref[...] + g_ref[...] * y_ref[...].astype(F32)
    if final:
        fg_ref, o_ref = rest
        x = x * lax.rsqrt(jnp.mean(x * x, axis=-1, keepdims=True) + EPS) * fg_ref[...]
    else:
        (o_ref,) = rest
    o_ref[...] = x


def _resid(x2, y2, mods3, mod_base, which, tiles_per_row_fn, final_g, tm):
    N, D = x2.shape
    tpr = tiles_per_row_fn(tm)
    final = final_g is not None
    in_specs = [pl.BlockSpec((tm, D), lambda i: (i, 0)), pl.BlockSpec((tm, D), lambda i: (i, 0)),
                _mod_spec(D, mod_base, which, tpr)]
    args = [x2, y2, mods3]
    if final:
        in_specs.append(pl.BlockSpec((1, D), lambda i: (0, 0)))
        args.append(final_g.reshape(1, D))
    return pl.pallas_call(
        functools.partial(_resid_kernel, final=final),
        grid=(N // tm,),
        in_specs=in_specs,
        out_specs=pl.BlockSpec((tm, D), lambda i: (i, 0)),
        out_shape=jax.ShapeDtypeStruct((N, D), F32),
        compiler_params=_cparams("parallel"),
        name="resid",
    )(*args)


def _rope_tables(S):
    t = jnp.arange(S)
    row_pos = (t // GRID_W).astype(F32)
    col_pos = (t % GRID_W).astype(F32)
    freqs = ROPE_THETA ** (-jnp.arange(ROPE_PAIRS, dtype=F32) / ROPE_PAIRS)
    lane = jnp.arange(LANES)
    within = lane % 32
    is_col = (lane % DIFF_QK_DIM) >= 32
    second = within >= ROPE_PAIRS
    pos = jnp.where(is_col[None, :], col_pos[:, None], row_pos[:, None])
    ang = pos * freqs[within % ROPE_PAIRS][None, :]
    cos, sin = jnp.cos(ang), jnp.sin(ang)
    sin_hi = jnp.where(second[None, :], sin, 0.0)
    sin_lo = jnp.where(second[None, :], 0.0, -sin)
    return cos, sin_hi, sin_lo


def kernel(x, c, ctx, c_ctx, ada_w, ada_b, norm1_g, norm2_g, w_in, pool_w, pool_scale, s5_a_re, s5_a_im, s5_log_dt, s5_b_re, s5_b_im, s5_c_re, s5_c_im, s5_d, s5_glu_w, diff_lq1, diff_lk1, diff_lq2, diff_lk2, diff_subln_g, w_branch_pool, w_branch_s5, w_branch_attn, w_out, peer_wq, peer_keys, peer_u, peer_v, final_g):
    B, S, D = x.shape
    CTX = ctx.shape[1]
    L = ada_w.shape[0]
    assert B + 1 <= MOD_ROWS and S % S5_CHUNK == 0 and CTX % S5_CHUNK == 0

    cc = jnp.concatenate([c, c_ctx[None, :], jnp.zeros((MOD_ROWS - B - 1, D), F32)], axis=0)
    mods3 = _ada(cc, ada_w, ada_b).reshape(L * MOD_ROWS * 6, 1, D)

    w_in_bf = w_in.astype(BF16)
    pool_w_bf = pool_w.astype(BF16)
    wglu_bf = s5_glu_w.astype(BF16)
    wbp_bf, wbs_bf, wba_bf = w_branch_pool.astype(BF16), w_branch_s5.astype(BF16), w_branch_attn.astype(BF16)
    wo_bf = w_out.astype(BF16)
    wq_bf = peer_wq.astype(BF16)
    keys_bf = peer_keys.astype(BF16).reshape(L, PEER_HEADS, 2 * PEER_N_KEYS, PEER_HALF)
    u_bf = peer_u.astype(BF16)
    vt_bf = jnp.swapaxes(peer_v.astype(BF16), 1, 2)
    n1g = norm1_g.reshape(L, 1, D)
    n2g = norm2_g.reshape(L, 1, D)
    pscale = pool_scale.reshape(L, 1, POOL_WIDTH)
    dsk = s5_d.reshape(L, 1, S5_WIDTH)
    lparams = tuple(a.reshape(L, 1, -1) for a in (diff_lq1, diff_lk1, diff_lq2, diff_lk2, diff_subln_g))
    rope_tabs = _rope_tables(S)

    lat_tpr = lambda tm: S // tm
    ctx_tpr = lambda tm: None

    xl = x.reshape(B * S, D)
    xc = ctx.reshape(B * CTX, D)
    n_lat, n_ctx = S // S5_CHUNK, CTX // S5_CHUNK

    tm_l = min(512, S)
    tm_c = min(512, CTX)
    tq_l, tk_l = min(256, S), min(512, S)
    tq_c = min(256, CTX)
    tn_peer = 512
    te_peer = 512
    tt_route = 256

    def peer(x2, mod_base, tpr, l, final):
        N = x2.shape[0]
        tm = min(512, N)
        h2, st = _peer_prep(x2, n2g, mods3, mod_base, tpr, wq_bf, keys_bf, l, tm)
        st4 = st.reshape(PEER_HEADS, 2, PEER_N_KEYS, N)
        thr, lk = _peer_route(st4, min(tt_route, N))
        y = _peer_dense(h2, u_bf, vt_bf, st4, thr, lk, l, min(tn_peer, N), te_peer)
        return _resid(x2, y, mods3, mod_base, 5, tpr, final_g if final else None, tm)

    for l in range(L):
        need_ctx = l < L - 1
        lam_init = 0.8 - 0.6 * math.exp(-0.3 * l)
        base_l = l * MOD_ROWS * 6
        base_c = base_l + B * 6

        zl = _inproj(xl, n1g, mods3, base_l, lat_tpr, w_in_bf, l, 0, N_COLS, tm_l, 1024)
        if need_ctx:
            c_lo = 0
            zc = _inproj(xc, n1g, mods3, base_c, ctx_tpr, w_in_bf, l, 0, N_COLS, tm_c, 1024)
        else:
            c_lo = COL_S5
            zc = _inproj(xc, n1g, mods3, base_c, ctx_tpr, w_in_bf, l, COL_S5, COL_GATE, tm_c, 512)
        zl3 = zl.reshape(B, S, -1)
        zc3 = zc.reshape(B, CTX, -1)

        a_l = _attention(zl3, COL_Q, zl3, COL_K, COL_V, zc3, COL_K - c_lo, COL_V - c_lo, rope_tabs, lparams, l,
                         lam_init, tq_l, tk_l)

        s5w = _s5_weights(s5_a_re[l], s5_a_im[l], s5_log_dt[l], s5_b_re[l], s5_b_im[l], s5_c_re[l], s5_c_im[l])
        s5c0 = (COL_S5 - c_lo) * LANES
        ug = jnp.concatenate([_to_chunks(zl3[:, :, COL_S5 * LANES:COL_S5 * LANES + S5_WIDTH]),
                              _to_chunks(zc3[:, :, s5c0:s5c0 + S5_WIDTH])], axis=1)
        yg = _s5_mix(ug, s5w, B, n_lat, n_lat + n_ctx)
        ys_l = _from_chunks(yg[:, :n_lat * B], B, S)

        p_l = _pool(zl3, pool_w_bf, pscale, l)
        m_l = _merge(p_l.reshape(B * S, -1), ys_l.reshape(B * S, -1), zl, COL_S5, COL_GATE, a_l.reshape(B * S, -1),
                     dsk, wglu_bf, wbp_bf, wbs_bf, wba_bf, l, min(256, S))
        xl = _resid_mm(m_l, wo_bf, xl, mods3, base_l, 2, lat_tpr, l, tm_l)
        xl = peer(xl, base_l, lat_tpr, l, l == L - 1)

        if need_ctx:
            a_c = _attention(zc3, COL_Q, zc3, COL_K, COL_V, None, 0, 0, None, lparams, l, lam_init, tq_c,
                             min(512, CTX))
            ys_c = _from_chunks(yg[:, n_lat * B:], B, CTX)
            p_c = _pool(zc3, pool_w_bf, pscale, l)
            m_c = _merge(p_c.reshape(B * CTX, -1), ys_c.reshape(B * CTX, -1), zc, COL_S5, COL_GATE,
                         a_c.reshape(B * CTX, -1), dsk, wglu_bf, wbp_bf, wbs_bf, wba_bf, l, min(256, CTX))
            xc = _resid_mm(m_c, wo_bf, xc, mods3, base_c, 2, ctx_tpr, l, tm_c)
            xc = peer(xc, base_c, ctx_tpr, l, False)

    return xl.reshape(B, S, D)
```

```python
import functools
import math

import jax
import jax.numpy as jnp
from jax import lax
from jax.experimental import pallas as pl
from jax.experimental.pallas import tpu as pltpu

F32 = jnp.float32
BF16 = jnp.bfloat16
EPS = 1e-6

GRID_W = 64
POOL_WINDOWS = (2, 4, 8, 16)
POOL_GROUP_W = 128
POOL_WIDTH = 512
S5_WIDTH = 512
S5_GROUP_W = 16
S5_GROUPS = 32
S5_STATE = 64
ATTN_WIDTH = 1024
DIFF_HEADS = 8
DIFF_V_DIM = 128
DIFF_QK_DIM = 64
ROPE_PAIRS = 16
ROPE_THETA = 10000.0
PEER_HEADS = 8
PEER_N_KEYS = 128
PEER_TOPK = 16
PEER_HALF = 128
COL_POOL = 0
COL_S5 = 4
COL_Q = 8
COL_K = 16
COL_V = 24
COL_GATE = 32
N_COLS = 80
MOD_ROWS = 16
S5_CHUNK = 16
NEG_INF = float("-inf")
LOG2E = math.log2(math.e)

LANES = 128
VMEM_LIMIT_BYTES = 56 * 1024 * 1024


def _cparams(*sem):
    return pltpu.CompilerParams(dimension_semantics=sem, vmem_limit_bytes=VMEM_LIMIT_BYTES)


def _dot(a, b):
    return jnp.dot(a, b, preferred_element_type=F32)


def _dot_nt(a, b):
    return lax.dot_general(a, b, (((1,), (1,)), ((), ())), preferred_element_type=F32)


def _ada_kernel(c_ref, w_ref, b_ref, o_ref):
    c = c_ref[...]
    s = c * jax.nn.sigmoid(c)
    o_ref[...] = _dot(s.astype(BF16), w_ref[...].astype(BF16)) + b_ref[...]


def _ada(cc, ada_w, ada_b):
    L, D, D6 = ada_w.shape
    tn = 1024
    return pl.pallas_call(
        _ada_kernel,
        grid=(L, D6 // tn),
        in_specs=[
            pl.BlockSpec((MOD_ROWS, D), lambda l, j: (0, 0)),
            pl.BlockSpec((None, D, tn), lambda l, j: (l, 0, j)),
            pl.BlockSpec((None, 1, tn), lambda l, j: (l, 0, j)),
        ],
        out_specs=pl.BlockSpec((None, MOD_ROWS, tn), lambda l, j: (l, 0, j)),
        out_shape=jax.ShapeDtypeStruct((L, MOD_ROWS, D6), F32),
        compiler_params=_cparams("parallel", "parallel"),
        name="ada",
    )(cc, ada_w, ada_b.reshape(L, 1, D6))


def _mod_spec(D, mod_base, which, tiles_per_row):
    if tiles_per_row is None:
        return pl.BlockSpec((None, 1, D), lambda i, *_: (mod_base + which, 0, 0))
    return pl.BlockSpec((None, 1, D), lambda i, *_: (mod_base + (i // tiles_per_row) * 6 + which, 0, 0))


def _norm_mod(x, g, shift, scale):
    ms = jnp.mean(x * x, axis=-1, keepdims=True)
    return (x * lax.rsqrt(ms + EPS) * g) * (1.0 + scale) + shift


def _inproj_kernel(x_ref, g_ref, sh_ref, sc_ref, w_ref, o_ref, h_scr):
    @pl.when(pl.program_id(1) == 0)
    def _():
        h_scr[...] = _norm_mod(x_ref[...], g_ref[...], sh_ref[...], sc_ref[...]).astype(BF16)

    o_ref[...] = _dot(h_scr[...], w_ref[...]).astype(o_ref.dtype)


def _inproj(x2, g, mods3, mod_base, tiles_per_row_fn, w_bf, layer, col_lo, col_hi, tm, tn):
    N, D = x2.shape
    ncol = (col_hi - col_lo) * LANES
    assert N % tm == 0 and ncol % tn == 0 and (col_lo * LANES) % tn == 0
    j0 = col_lo * LANES // tn
    tpr = tiles_per_row_fn(tm)
    return pl.pallas_call(
        _inproj_kernel,
        grid=(N // tm, ncol // tn),
        in_specs=[
            pl.BlockSpec((tm, D), lambda i, j: (i, 0)),
            pl.BlockSpec((None, 1, D), lambda i, j: (layer, 0, 0)),
            _mod_spec(D, mod_base, 0, tpr),
            _mod_spec(D, mod_base, 1, tpr),
            pl.BlockSpec((None, D, tn), lambda i, j: (layer, 0, j0 + j)),
        ],
        out_specs=pl.BlockSpec((tm, tn), lambda i, j: (i, j)),
        out_shape=jax.ShapeDtypeStruct((N, ncol), BF16),
        scratch_shapes=[pltpu.VMEM((tm, D), BF16)],
        compiler_params=_cparams("parallel", "arbitrary"),
        name="inproj",
    )(x2, g, mods3, mods3, w_bf)


def _rope(x, cos, sin_hi, sin_lo):
    return x * cos + pltpu.roll(x, 16, 1) * sin_hi + pltpu.roll(x, LANES - 16, 1) * sin_lo


def _rope_k_kernel(k_ref, cos_ref, sh_ref, sl_ref, o_ref):
    cos, sh, sl = cos_ref[...], sh_ref[...], sl_ref[...]
    for h in range(DIFF_HEADS):
        cols = slice(h * LANES, (h + 1) * LANES)
        o_ref[:, cols] = _rope(k_ref[:, cols].astype(F32), cos, sh, sl).astype(o_ref.dtype)


def _rope_k(z2, S, rope_tabs, tm):
    N = z2.shape[0]
    per_seq = S // tm
    tab = pl.BlockSpec((tm, LANES), lambda i: (i % per_seq, 0))
    return pl.pallas_call(
        _rope_k_kernel,
        grid=(N // tm,),
        in_specs=[pl.BlockSpec((tm, ATTN_WIDTH), lambda i: (i, COL_K * LANES // ATTN_WIDTH)), tab, tab, tab],
        out_specs=pl.BlockSpec((tm, ATTN_WIDTH), lambda i: (i, 0)),
        out_shape=jax.ShapeDtypeStruct((N, ATTN_WIDTH), BF16),
        compiler_params=_cparams("parallel"),
        name="rope_k",
    )(z2, *rope_tabs)


def _attn_kernel(*refs, use_rope, has_ctx, lam_init):
    it = iter(refs)
    q_ref, kl_ref, vl_ref = next(it), next(it), next(it)
    if has_ctx:
        kc_ref, vc_ref = next(it), next(it)
    if use_rope:
        cosq_ref, shq_ref, slq_ref = next(it), next(it), next(it)
    lq1_ref, lk1_ref, lq2_ref, lk2_ref, g_ref = (next(it) for _ in range(5))
    o_ref = next(it)
    pl_scr = next(it)
    if has_ctx:
        pc_scr = next(it)
    linv_scr = next(it)
    tq = q_ref.shape[0]

    step = pl.program_id(0)
    cur = step % 2
    prev = 1 - cur

    @pl.when(step == 0)
    def _():
        pl_scr[...] = jnp.zeros_like(pl_scr)
        if has_ctx:
            pc_scr[...] = jnp.zeros_like(pc_scr)
        linv_scr[...] = jnp.zeros_like(linv_scr)

    lam = (jnp.exp(jnp.sum(lq1_ref[...] * lk1_ref[...], axis=-1, keepdims=True))
           - jnp.exp(jnp.sum(lq2_ref[...] * lk2_ref[...], axis=-1, keepdims=True)) + lam_init)

    o = _dot(pl_scr[prev], vl_ref[...])
    if has_ctx:
        o = o + _dot(pc_scr[prev], vc_ref[...])
    o = o * linv_scr[prev]
    o = o[:tq] - lam * o[tq:]
    o = o * lax.rsqrt(jnp.mean(o * o, axis=-1, keepdims=True) + EPS) * g_ref[...]
    o_ref[...] = (o * (1.0 - lam_init)).astype(o_ref.dtype)

    q = q_ref[...].astype(F32)
    if use_rope:
        q = _rope(q, cosq_ref[...], shq_ref[...], slq_ref[...])
    q = q * (DIFF_QK_DIM ** -0.5 * LOG2E)
    lane = lax.broadcasted_iota(jnp.int32, q.shape, 1)
    qs = jnp.concatenate([jnp.where(lane < DIFF_QK_DIM, q, 0.0), jnp.where(lane >= DIFF_QK_DIM, q, 0.0)],
                         axis=0).astype(BF16)
    s_l = _dot_nt(qs, kl_ref[...])
    m = jnp.max(s_l, axis=-1, keepdims=True)
    if has_ctx:
        s_c = _dot_nt(qs, kc_ref[...])
        m = jnp.maximum(m, jnp.max(s_c, axis=-1, keepdims=True))
    p_l = jnp.exp2(s_l - m)
    l = jnp.sum(p_l, axis=-1, keepdims=True)
    pl_scr[cur] = p_l.astype(BF16)
    if has_ctx:
        p_c = jnp.exp2(s_c - m)
        l = l + jnp.sum(p_c, axis=-1, keepdims=True)
        pc_scr[cur] = p_c.astype(BF16)
    linv_scr[cur] = jnp.broadcast_to(1.0 / l, (2 * tq, LANES))


def _attention(zq3, q_col, zk3, k_col, zv3, v_col, zc3, kc_col, vc_col, rope_tabs, lparams, layer, lam_init, tq):
    B, Sq, _ = zq3.shape
    Sk = zk3.shape[1]
    use_rope = rope_tabs is not None
    has_ctx = zc3 is not None
    assert Sq % tq == 0
    H = DIFF_HEADS
    nq = Sq // tq
    n_tiles = B * H * nq

    def tile(step, lag):
        t = jnp.clip(step - lag, 0, n_tiles - 1)
        return t // (H * nq), (t // nq) % H, t % nq

    def spec(rows, col, lag, per_tile):
        def index(step):
            b, h, i = tile(step, lag)
            return (b, i if per_tile else 0, col + h)
        return pl.BlockSpec((None, rows, LANES), index)

    in_specs = [spec(tq, q_col, 0, True), spec(Sk, k_col, 0, False), spec(Sk, v_col, 1, False)]
    args = [zq3, zk3, zv3]
    scratch = [pltpu.VMEM((2, 2 * tq, Sk), BF16)]
    if has_ctx:
        n_ctx = zc3.shape[1]
        in_specs += [spec(n_ctx, kc_col, 0, False), spec(n_ctx, vc_col, 1, False)]
        args += [zc3, zc3]
        scratch.append(pltpu.VMEM((2, 2 * tq, n_ctx), BF16))
    scratch.append(pltpu.VMEM((2, 2 * tq, LANES), F32))
    if use_rope:
        in_specs += [pl.BlockSpec((tq, LANES), lambda step: (tile(step, 0)[2], 0))] * 3
        args += list(rope_tabs)
    in_specs += [pl.BlockSpec((None, 1, DIFF_QK_DIM), lambda step: (layer, 0, 0))] * 4
    in_specs += [pl.BlockSpec((None, 1, DIFF_V_DIM), lambda step: (layer, 0, 0))]
    args += list(lparams)

    def out_index(step):
        b, h, i = tile(step, 1)
        return (b, i, h)

    return pl.pallas_call(
        functools.partial(_attn_kernel, use_rope=use_rope, has_ctx=has_ctx, lam_init=lam_init),
        grid=(n_tiles + 1,),
        in_specs=in_specs,
        out_specs=pl.BlockSpec((None, tq, LANES), out_index),
        out_shape=jax.ShapeDtypeStruct((B, Sq, ATTN_WIDTH), BF16),
        scratch_shapes=scratch,
        compiler_params=_cparams("arbitrary"),
        name="attn",
    )(*args)


def _s5_weights(a_re, a_im, log_dt, b_re, b_im, c_re, c_im):
    T = S5_CHUNK
    hp = lax.Precision.HIGHEST
    ks = jnp.arange(T + 1, dtype=F32)
    mt, wc, vc, la, lb = [], [], [], [], []
    per_dir = []
    for d in range(2):
        dt = jnp.exp(log_dt[d])[:, None]
        are, aim = a_re[d], a_im[d]
        mag = jnp.exp(are * dt)
        lb_re, lb_im = mag * jnp.cos(aim * dt), mag * jnp.sin(aim * dt)
        den = are * are + aim * aim
        coef_re = ((lb_re - 1.0) * are + lb_im * aim) / den
        coef_im = (lb_im * are - (lb_re - 1.0) * aim) / den
        bb_re = coef_re[..., None] * b_re[d] - coef_im[..., None] * b_im[d]
        bb_im = coef_re[..., None] * b_im[d] + coef_im[..., None] * b_re[d]
        pmag = jnp.exp(ks[:, None, None] * (are * dt)[None])
        pang = ks[:, None, None] * (aim * dt)[None]
        p_re, p_im = pmag * jnp.cos(pang), pmag * jnp.sin(pang)
        cl_re = c_re[d][None] * p_re[:, :, None, :] - c_im[d][None] * p_im[:, :, None, :]
        cl_im = c_re[d][None] * p_im[:, :, None, :] + c_im[d][None] * p_re[:, :, None, :]
        kk = (jnp.einsum('kgip,gph->kgih', cl_re, bb_re, precision=hp)
              - jnp.einsum('kgip,gph->kgih', cl_im, bb_im, precision=hp))
        lbb_re = p_re[..., None] * bb_re[None] - p_im[..., None] * bb_im[None]
        lbb_im = p_re[..., None] * bb_im[None] + p_im[..., None] * bb_re[None]
        per_dir.append((kk, lbb_re, lbb_im, cl_re, cl_im, p_re[T], p_im[T]))

    G = a_re.shape[1]
    P = a_re.shape[2]
    Hc = S5_GROUP_W
    j = jnp.arange(T)[:, None]
    t = jnp.arange(T)[None, :]
    kf, kb = per_dir[0][0], per_dir[1][0]
    mf = jnp.where((t >= j)[:, :, None, None, None], kf[jnp.clip(t - j, 0, T)], 0.0)
    mb = jnp.where((j >= t)[:, :, None, None, None], kb[jnp.clip(j - t, 0, T)], 0.0)
    mtot = jnp.transpose(mf + mb, (2, 0, 4, 1, 3)).reshape(G, T * Hc, T * Hc)

    def wpart(lbb, order):
        return jnp.transpose(lbb[order], (1, 0, 3, 2)).reshape(G, T * Hc, P)

    f_ord = (T - 1) - jnp.arange(T)
    b_ord = jnp.arange(T)
    wcat = jnp.concatenate([wpart(per_dir[0][1], f_ord), wpart(per_dir[1][1], b_ord),
                            wpart(per_dir[0][2], f_ord), wpart(per_dir[1][2], b_ord)], axis=-1)

    def vpart(cl, order, sign):
        return sign * jnp.transpose(cl[order], (1, 3, 0, 2)).reshape(G, P, T * Hc)

    fv_ord = jnp.arange(T) + 1
    bv_ord = T - jnp.arange(T)
    vcat = jnp.concatenate([vpart(per_dir[0][3], fv_ord, 1.0), vpart(per_dir[1][3], bv_ord, 1.0),
                            vpart(per_dir[0][4], fv_ord, -1.0), vpart(per_dir[1][4], bv_ord, -1.0)], axis=1)
    la = jnp.concatenate([per_dir[0][5], per_dir[1][5]], axis=-1)[:, None, :]
    lb = jnp.concatenate([per_dir[0][6], per_dir[1][6]], axis=-1)[:, None, :]
    return mtot.astype(BF16), wcat.astype(BF16), vcat.astype(BF16), la, lb


def _s5_kernel(u_ref, mtot_ref, wcat_ref, vcat_ref, la_ref, lb_ref, y_ref, sc_scr, hs_scr, *, rows, n_lat, n_all):
    P = S5_STATE
    u = u_ref[...]
    sc_scr[...] = _dot(u, wcat_ref[...])
    la = la_ref[...]
    lb = lb_ref[...]
    fwd = lax.broadcasted_iota(jnp.int32, (rows, 2 * P), 1) < P

    def body(k, carry):
        a, b = carry
        kf = k + n_lat
        kf = jnp.where(kf >= n_all, kf - n_all, kf)
        rf = pl.ds(pl.multiple_of(kf * rows, rows), rows)
        rb = pl.ds(pl.multiple_of((n_all - 1 - k) * rows, rows), rows)
        hs_scr[rf, 0:P] = a[:, 0:P]
        hs_scr[rb, P:2 * P] = a[:, P:2 * P]
        hs_scr[rf, 2 * P:3 * P] = b[:, 0:P]
        hs_scr[rb, 3 * P:4 * P] = b[:, P:2 * P]
        sa = jnp.where(fwd, sc_scr[rf, 0:2 * P], sc_scr[rb, 0:2 * P])
        sb = jnp.where(fwd, sc_scr[rf, 2 * P:4 * P], sc_scr[rb, 2 * P:4 * P])
        return la * a - lb * b + sa, la * b + lb * a + sb

    zero = jnp.zeros((rows, 2 * P), F32)
    lax.fori_loop(0, n_all, body, (zero, zero))
    y_ref[...] = _dot(u, mtot_ref[...]) + _dot(hs_scr[...].astype(BF16), vcat_ref[...])


def _s5_mix(ug, weights, rows, n_lat, n_all):
    mtot, wcat, vcat, la, lb = weights
    G, R, W = ug.shape
    mat = pl.BlockSpec((None, W, W), lambda g: (g, 0, 0))
    vec = pl.BlockSpec((None, 1, 2 * S5_STATE), lambda g: (g, 0, 0))
    return pl.pallas_call(
        functools.partial(_s5_kernel, rows=rows, n_lat=n_lat, n_all=n_all),
        grid=(G,),
        in_specs=[pl.BlockSpec((None, R, W), lambda g: (g, 0, 0)), mat, mat, mat, vec, vec],
        out_specs=pl.BlockSpec((None, R, W), lambda g: (g, 0, 0)),
        out_shape=jax.ShapeDtypeStruct((G, R, W), F32),
        scratch_shapes=[pltpu.VMEM((R, W), F32), pltpu.VMEM((R, W), F32)],
        compiler_params=_cparams("parallel"),
        name="s5",
    )(ug, mtot, wcat, vcat, la, lb)


def _to_chunks(u3):
    B, S, _ = u3.shape
    u = u3.reshape(B, S // S5_CHUNK, S5_CHUNK, S5_GROUPS, S5_GROUP_W)
    return jnp.transpose(u, (3, 1, 0, 2, 4)).reshape(S5_GROUPS, (S // S5_CHUNK) * B, S5_CHUNK * S5_GROUP_W)


def _from_chunks(y, B, S):
    y = y.reshape(S5_GROUPS, S // S5_CHUNK, B, S5_CHUNK, S5_GROUP_W)
    return jnp.transpose(y, (2, 1, 3, 0, 4)).reshape(B, S, S5_WIDTH)


POOL_PAD = 16


def _pool_kernel(p_ref, w_ref, scale_ref, o_ref, pad_scr, *, seq, chunk):
    pad_scr[0:POOL_PAD, :] = jnp.zeros((POOL_PAD, POOL_WIDTH), F32)
    pad_scr[POOL_PAD + seq:POOL_PAD + seq + POOL_PAD, :] = jnp.zeros((POOL_PAD, POOL_WIDTH), F32)
    pad_scr[POOL_PAD:POOL_PAD + seq, :] = p_ref[...].astype(F32)
    for r0 in range(0, seq, chunk):
        t = r0 + lax.broadcasted_iota(jnp.int32, (chunk, 1), 0)
        for g, win in enumerate(POOL_WINDOWS):
            cols = slice(g * POOL_GROUP_W, (g + 1) * POOL_GROUP_W)
            lo_off, hi_off = -(win // 2), win - 1 - win // 2
            acc = None
            for k in range(lo_off, hi_off + 1):
                v = pad_scr[POOL_PAD + r0 + k:POOL_PAD + r0 + k + chunk, cols]
                acc = v if acc is None else acc + v
            lo = jnp.clip(t + lo_off, 0, seq - 1)
            hi = jnp.clip(t + hi_off, 0, seq - 1)
            cnt = (hi - lo + 1).astype(F32)
            d = acc / cnt - pad_scr[POOL_PAD + r0:POOL_PAD + r0 + chunk, cols]
            y = _dot(d.astype(BF16), w_ref[g]) * scale_ref[:, cols]
            o_ref[r0:r0 + chunk, cols] = y.astype(o_ref.dtype)


def _pool(z3, w_bf, scale, layer):
    B, S, _ = z3.shape
    chunk = min(512, S)
    return pl.pallas_call(
        functools.partial(_pool_kernel, seq=S, chunk=chunk),
        grid=(B,),
        in_specs=[
            pl.BlockSpec((None, S, POOL_WIDTH), lambda b: (b, 0, COL_POOL)),
            pl.BlockSpec((None, len(POOL_WINDOWS), POOL_GROUP_W, POOL_GROUP_W), lambda b: (layer, 0, 0, 0)),
            pl.BlockSpec((None, 1, POOL_WIDTH), lambda b: (layer, 0, 0)),
        ],
        out_specs=pl.BlockSpec((None, S, POOL_WIDTH), lambda b: (b, 0, 0)),
        out_shape=jax.ShapeDtypeStruct((B, S, POOL_WIDTH), BF16),
        scratch_shapes=[pltpu.VMEM((S + 2 * POOL_PAD, POOL_WIDTH), F32)],
        compiler_params=_cparams("parallel"),
        name="pool",
    )(z3, w_bf, scale)


def _merge_kernel(p_ref, ys_ref, u_ref, a_ref, gp_ref, gs_ref, ga_ref, dsk_ref, wglu_ref, wbp_ref, wbs_ref,
                  wba_ref, m_ref):
    y = u_ref[...].astype(F32) * dsk_ref[...] + ys_ref[...]
    g = jax.nn.gelu(y)
    s5 = g * jax.nn.sigmoid(_dot(g.astype(BF16), wglu_ref[...]))
    m = jax.nn.sigmoid(gp_ref[...].astype(F32)) * _dot(p_ref[...], wbp_ref[...])
    m = m + jax.nn.sigmoid(gs_ref[...].astype(F32)) * _dot(s5.astype(BF16), wbs_ref[...])
    m = m + jax.nn.sigmoid(ga_ref[...].astype(F32)) * _dot(a_ref[...], wba_ref[...])
    m_ref[...] = m.astype(m_ref.dtype)


def _merge(p2, ys2, z2, s5_col, gate_col, a2, dsk, wglu, wbp, wbs, wba, layer, tm):
    N = p2.shape[0]
    D = wbp.shape[-1]
    dc = D // LANES

    def wspec(k, n):
        return pl.BlockSpec((None, k, n), lambda i: (layer, 0, 0))

    return pl.pallas_call(
        _merge_kernel,
        grid=(N // tm,),
        in_specs=[
            pl.BlockSpec((tm, POOL_WIDTH), lambda i: (i, 0)),
            pl.BlockSpec((tm, S5_WIDTH), lambda i: (i, 0)),
            pl.BlockSpec((tm, S5_WIDTH), lambda i: (i, s5_col // 4)),
            pl.BlockSpec((tm, ATTN_WIDTH), lambda i: (i, 0)),
            pl.BlockSpec((tm, D), lambda i: (i, gate_col // dc)),
            pl.BlockSpec((tm, D), lambda i: (i, gate_col // dc + 1)),
            pl.BlockSpec((tm, D), lambda i: (i, gate_col // dc + 2)),
            pl.BlockSpec((None, 1, S5_WIDTH), lambda i: (layer, 0, 0)),
            wspec(S5_WIDTH, S5_WIDTH), wspec(POOL_WIDTH, D), wspec(S5_WIDTH, D), wspec(ATTN_WIDTH, D),
        ],
        out_specs=pl.BlockSpec((tm, D), lambda i: (i, 0)),
        out_shape=jax.ShapeDtypeStruct((N, D), BF16),
        compiler_params=_cparams("parallel"),
        name="merge",
    )(p2, ys2, z2, a2, z2, z2, z2, dsk, wglu, wbp, wbs, wba)


def _resid_mm_kernel(m_ref, w_ref, x_ref, g_ref, o_ref):
    o_ref[...] = x_ref[...] + g_ref[...] * _dot(m_ref[...], w_ref[...])


def _resid_mm(m2, w_bf, x2, mods3, mod_base, which, tiles_per_row_fn, layer, tm):
    N, D = x2.shape
    K = m2.shape[1]
    tpr = tiles_per_row_fn(tm)
    return pl.pallas_call(
        _resid_mm_kernel,
        grid=(N // tm,),
        in_specs=[
            pl.BlockSpec((tm, K), lambda i: (i, 0)),
            pl.BlockSpec((None, K, D), lambda i: (layer, 0, 0)),
            pl.BlockSpec((tm, D), lambda i: (i, 0)),
            _mod_spec(D, mod_base, which, tpr),
        ],
        out_specs=pl.BlockSpec((tm, D), lambda i: (i, 0)),
        out_shape=jax.ShapeDtypeStruct((N, D), F32),
        compiler_params=_cparams("parallel"),
        name="resid_mm",
    )(m2, w_bf, x2, mods3)


def _peer_prep_kernel(x_ref, g_ref, sh_ref, sc_ref, wq_ref, keys_ref, h_ref, st_ref):
    @pl.when(pl.program_id(1) == 0)
    def _():
        h_ref[...] = _norm_mod(x_ref[...], g_ref[...], sh_ref[...], sc_ref[...]).astype(BF16)

    q = _dot(h_ref[...], wq_ref[...]).astype(BF16)
    for c in range(2):
        qc = q[:, c * PEER_HALF:(c + 1) * PEER_HALF]
        kc = keys_ref[c * PEER_N_KEYS:(c + 1) * PEER_N_KEYS, :]
        st_ref[c * PEER_N_KEYS:(c + 1) * PEER_N_KEYS, :] = _dot_nt(kc, qc) * LOG2E


def _peer_prep(x2, g, mods3, mod_base, tiles_per_row_fn, wq_bf, keys_bf, layer, tm):
    N, D = x2.shape
    tpr = tiles_per_row_fn(tm)
    hw = 2 * PEER_HALF
    return pl.pallas_call(
        _peer_prep_kernel,
        grid=(N // tm, PEER_HEADS),
        in_specs=[
            pl.BlockSpec((tm, D), lambda i, j: (i, 0)),
            pl.BlockSpec((None, 1, D), lambda i, j: (layer, 0, 0)),
            _mod_spec(D, mod_base, 3, tpr),
            _mod_spec(D, mod_base, 4, tpr),
            pl.BlockSpec((None, D, hw), lambda i, j: (layer, 0, j)),
            pl.BlockSpec((None, None, 2 * PEER_N_KEYS, PEER_HALF), lambda i, j: (layer, j, 0, 0)),
        ],
        out_specs=[pl.BlockSpec((tm, D), lambda i, j: (i, 0)),
                   pl.BlockSpec((2 * PEER_N_KEYS, tm), lambda i, j: (j, i))],
        out_shape=[jax.ShapeDtypeStruct((N, D), BF16),
                   jax.ShapeDtypeStruct((PEER_HEADS * 2 * PEER_N_KEYS, N), F32)],
        compiler_params=_cparams("parallel", "arbitrary"),
        name="peer_prep",
    )(x2, g, mods3, mods3, wq_bf, keys_bf)


def _extract_top(cur, n):
    rows = cur.shape[0]
    ridx = lax.broadcasted_iota(jnp.int32, cur.shape, 0).astype(F32)
    out = []
    for _ in range(n):
        m = jnp.max(cur, axis=0, keepdims=True)
        out.append(m)
        first = jnp.min(jnp.where(cur == m, ridx, float(rows)), axis=0, keepdims=True)
        cur = jnp.where(ridx == first, NEG_INF, cur)
    return out


_CAND_PAIRS = tuple((j, k) for j in range(PEER_TOPK + 1) for k in range(PEER_TOPK + 1)
                    if (j + 1) * (k + 1) <= PEER_TOPK + 1)


_N_CAND = -(-len(_CAND_PAIRS) // 8) * 8


def _peer_route_kernel(st_ref, thr_ref, lk_ref, cand_scr):
    tt = st_ref.shape[-1]
    cand_scr[...] = jnp.full((_N_CAND, tt), NEG_INF, F32)
    for h in range(PEER_HEADS):
        s1 = st_ref[h, 0]
        s2 = st_ref[h, 1]
        v1 = _extract_top(s1, PEER_TOPK + 1)
        v2 = _extract_top(s2, PEER_TOPK + 1)
        for r, (j, k) in enumerate(_CAND_PAIRS):
            cand_scr[r:r + 1, :] = v1[j] + v2[k]
        cand = cand_scr[...]
        top = _extract_top(cand, PEER_TOPK + 1)
        tau = 0.5 * (top[PEER_TOPK - 1] + top[PEER_TOPK])
        mx = top[0]
        z = jnp.sum(jnp.where(cand > tau, jnp.exp2(cand - mx), 0.0), axis=0, keepdims=True)
        lk = tau - mx - jnp.log2(z)
        thr_ref[h] = (tau - s1) - lk
        lk_ref[h:h + 1, :] = lk


def _peer_route(st4, tt):
    H, _, nk, N = st4.shape
    return pl.pallas_call(
        _peer_route_kernel,
        grid=(N // tt,),
        in_specs=[pl.BlockSpec((H, 2, nk, tt), lambda i: (0, 0, 0, i))],
        out_specs=[pl.BlockSpec((H, nk, tt), lambda i: (0, 0, i)),
                   pl.BlockSpec((H, tt), lambda i: (0, i))],
        out_shape=[jax.ShapeDtypeStruct((H, nk, N), F32), jax.ShapeDtypeStruct((H, N), F32)],
        scratch_shapes=[pltpu.VMEM((_N_CAND, tt), F32)],
        compiler_params=_cparams("parallel"),
        name="peer_route",
    )(st4)


def _peer_dense_kernel(h_ref, u_ref, vt_ref, s2_ref, thr_ref, lk_ref, o_ref, acc, act_scr, w_scr, *, n_sub, n_e,
                       n_pairs):
    step = pl.program_id(0)
    cur = step % 2
    prev = 1 - cur
    e_out = jnp.clip(step - 2, 0, n_pairs - 1) % n_e
    e_w = jnp.clip(step - 1, 0, n_pairs - 1) % n_e

    @pl.when(step == 0)
    def _():
        act_scr[...] = jnp.zeros_like(act_scr)
        w_scr[...] = jnp.zeros_like(w_scr)

    @pl.when(e_out == 0)
    def _():
        acc[...] = jnp.zeros_like(acc)

    acc[...] += _dot(vt_ref[...], w_scr[cur])

    nk = PEER_N_KEYS
    for i in range(n_sub):
        i1 = e_w * n_sub + i
        gate = None
        for h in range(PEER_HEADS):
            lg = s2_ref[h] - thr_ref[h, pl.ds(i1, 1), :]
            term = jnp.exp2(jnp.where(lg > lk_ref[h:h + 1, :], lg, NEG_INF))
            gate = term if gate is None else gate + term
        rows = slice(i * nk, (i + 1) * nk)
        w_scr[prev, rows, :] = (gate * jax.nn.gelu(act_scr[prev, rows, :])).astype(BF16)

    act_scr[cur] = _dot_nt(u_ref[...], h_ref[...])

    @pl.when((e_out == n_e - 1) & (step >= 2))
    def _():
        o_ref[...] = acc[...].T.astype(o_ref.dtype)


def _peer_dense(h2, u_bf, vt_bf, st4, thr, lk, layer, tn, te):
    N, D = h2.shape
    H, _, nk, _ = st4.shape
    E = u_bf.shape[1]
    n_e = E // te
    n_pairs = (N // tn) * n_e

    def pair(step, lag):
        s = jnp.clip(step - lag, 0, n_pairs - 1)
        return s // n_e, s % n_e

    return pl.pallas_call(
        functools.partial(_peer_dense_kernel, n_sub=te // nk, n_e=n_e, n_pairs=n_pairs),
        grid=(n_pairs + 2,),
        in_specs=[
            pl.BlockSpec((tn, D), lambda g: (pair(g, 0)[0], 0)),
            pl.BlockSpec((None, te, D), lambda g: (layer, pair(g, 0)[1], 0)),
            pl.BlockSpec((None, D, te), lambda g: (layer, 0, pair(g, 2)[1])),
            pl.BlockSpec((H, None, nk, tn), lambda g: (0, 1, 0, pair(g, 1)[0])),
            pl.BlockSpec((H, nk, tn), lambda g: (0, 0, pair(g, 1)[0])),
            pl.BlockSpec((H, tn), lambda g: (0, pair(g, 1)[0])),
        ],
        out_specs=pl.BlockSpec((tn, D), lambda g: (pair(g, 2)[0], 0)),
        out_shape=jax.ShapeDtypeStruct((N, D), BF16),
        scratch_shapes=[pltpu.VMEM((D, tn), F32), pltpu.VMEM((2, te, tn), F32), pltpu.VMEM((2, te, tn), BF16)],
        compiler_params=_cparams("arbitrary"),
        name="peer_dense",
    )(h2, u_bf, vt_bf, st4, thr, lk)


def _resid_kernel(x_ref, y_ref, g_ref, *rest, final):
    x = x_ref[...] + g_ref[...] * y_ref[...].astype(F32)
    if final:
        fg_ref, o_ref = rest
        x = x * lax.rsqrt(jnp.mean(x * x, axis=-1, keepdims=True) + EPS) * fg_ref[...]
    else:
        (o_ref,) = rest
    o_ref[...] = x


def _resid(x2, y2, mods3, mod_base, which, tiles_per_row_fn, final_g, tm):
    N, D = x2.shape
    tpr = tiles_per_row_fn(tm)
    final = final_g is not None
    in_specs = [pl.BlockSpec((tm, D), lambda i: (i, 0)), pl.BlockSpec((tm, D), lambda i: (i, 0)),
                _mod_spec(D, mod_base, which, tpr)]
    args = [x2, y2, mods3]
    if final:
        in_specs.append(pl.BlockSpec((1, D), lambda i: (0, 0)))
        args.append(final_g.reshape(1, D))
    return pl.pallas_call(
        functools.partial(_resid_kernel, final=final),
        grid=(N // tm,),
        in_specs=in_specs,
        out_specs=pl.BlockSpec((tm, D), lambda i: (i, 0)),
        out_shape=jax.ShapeDtypeStruct((N, D), F32),
        compiler_params=_cparams("parallel"),
        name="resid",
    )(*args)


def _rope_tables(S):
    t = jnp.arange(S)
    row_pos = (t // GRID_W).astype(F32)
    col_pos = (t % GRID_W).astype(F32)
    freqs = ROPE_THETA ** (-jnp.arange(ROPE_PAIRS, dtype=F32) / ROPE_PAIRS)
    lane = jnp.arange(LANES)
    within = lane % 32
    is_col = (lane % DIFF_QK_DIM) >= 32
    second = within >= ROPE_PAIRS
    pos = jnp.where(is_col[None, :], col_pos[:, None], row_pos[:, None])
    ang = pos * freqs[within % ROPE_PAIRS][None, :]
    cos, sin = jnp.cos(ang), jnp.sin(ang)
    sin_hi = jnp.where(second[None, :], sin, 0.0)
    sin_lo = jnp.where(second[None, :], 0.0, -sin)
    return cos, sin_hi, sin_lo


def kernel(x, c, ctx, c_ctx, ada_w, ada_b, norm1_g, norm2_g, w_in, pool_w, pool_scale, s5_a_re, s5_a_im, s5_log_dt, s5_b_re, s5_b_im, s5_c_re, s5_c_im, s5_d, s5_glu_w, diff_lq1, diff_lk1, diff_lq2, diff_lk2, diff_subln_g, w_branch_pool, w_branch_s5, w_branch_attn, w_out, peer_wq, peer_keys, peer_u, peer_v, final_g):
    B, S, D = x.shape
    CTX = ctx.shape[1]
    L = ada_w.shape[0]
    assert B + 1 <= MOD_ROWS and S % S5_CHUNK == 0 and CTX % S5_CHUNK == 0

    cc = jnp.concatenate([c, c_ctx[None, :], jnp.zeros((MOD_ROWS - B - 1, D), F32)], axis=0)
    mods3 = _ada(cc, ada_w, ada_b).reshape(L * MOD_ROWS * 6, 1, D)

    w_in_bf = w_in.astype(BF16)
    pool_w_bf = pool_w.astype(BF16)
    wglu_bf = s5_glu_w.astype(BF16)
    wbp_bf, wbs_bf, wba_bf = w_branch_pool.astype(BF16), w_branch_s5.astype(BF16), w_branch_attn.astype(BF16)
    wo_bf = w_out.astype(BF16)
    wq_bf = peer_wq.astype(BF16)
    keys_bf = peer_keys.astype(BF16).reshape(L, PEER_HEADS, 2 * PEER_N_KEYS, PEER_HALF)
    u_bf = peer_u.astype(BF16)
    vt_bf = jnp.swapaxes(peer_v.astype(BF16), 1, 2)
    n1g = norm1_g.reshape(L, 1, D)
    n2g = norm2_g.reshape(L, 1, D)
    pscale = pool_scale.reshape(L, 1, POOL_WIDTH)
    dsk = s5_d.reshape(L, 1, S5_WIDTH)
    lparams = tuple(a.reshape(L, 1, -1) for a in (diff_lq1, diff_lk1, diff_lq2, diff_lk2, diff_subln_g))
    rope_tabs = _rope_tables(S)

    lat_tpr = lambda tm: S // tm
    ctx_tpr = lambda tm: None

    xl = x.reshape(B * S, D)
    xc = ctx.reshape(B * CTX, D)
    n_lat, n_ctx = S // S5_CHUNK, CTX // S5_CHUNK

    tm_l = min(512, S)
    tm_c = min(512, CTX)
    tq_l = min(256, S)
    tq_c = min(256, CTX)
    tn_peer = 512
    te_peer = 1024
    tt_route = 256

    def peer(x2, mod_base, tpr, l, final):
        N = x2.shape[0]
        tm = min(512, N)
        h2, st = _peer_prep(x2, n2g, mods3, mod_base, tpr, wq_bf, keys_bf, l, tm)
        st4 = st.reshape(PEER_HEADS, 2, PEER_N_KEYS, N)
        thr, lk = _peer_route(st4, min(tt_route, N))
        y = _peer_dense(h2, u_bf, vt_bf, st4, thr, lk, l, min(tn_peer, N), te_peer)
        return _resid(x2, y, mods3, mod_base, 5, tpr, final_g if final else None, tm)

    for l in range(L):
        need_ctx = l < L - 1
        lam_init = 0.8 - 0.6 * math.exp(-0.3 * l)
        base_l = l * MOD_ROWS * 6
        base_c = base_l + B * 6

        zl = _inproj(xl, n1g, mods3, base_l, lat_tpr, w_in_bf, l, 0, N_COLS, tm_l, 1024)
        if need_ctx:
            c_lo = 0
            zc = _inproj(xc, n1g, mods3, base_c, ctx_tpr, w_in_bf, l, 0, N_COLS, tm_c, 1024)
        else:
            c_lo = COL_S5
            zc = _inproj(xc, n1g, mods3, base_c, ctx_tpr, w_in_bf, l, COL_S5, COL_GATE, tm_c, 512)
        zl3 = zl.reshape(B, S, -1)
        zc3 = zc.reshape(B, CTX, -1)

        kr3 = _rope_k(zl, S, rope_tabs, tm_l).reshape(B, S, ATTN_WIDTH)
        a_l = _attention(zl3, COL_Q, kr3, 0, zl3, COL_V, zc3, COL_K - c_lo, COL_V - c_lo, rope_tabs, lparams, l,
                         lam_init, tq_l)

        s5w = _s5_weights(s5_a_re[l], s5_a_im[l], s5_log_dt[l], s5_b_re[l], s5_b_im[l], s5_c_re[l], s5_c_im[l])
        s5c0 = (COL_S5 - c_lo) * LANES
        ug = jnp.concatenate([_to_chunks(zl3[:, :, COL_S5 * LANES:COL_S5 * LANES + S5_WIDTH]),
                              _to_chunks(zc3[:, :, s5c0:s5c0 + S5_WIDTH])], axis=1)
        yg = _s5_mix(ug, s5w, B, n_lat, n_lat + n_ctx)
        ys_l = _from_chunks(yg[:, :n_lat * B], B, S)

        p_l = _pool(zl3, pool_w_bf, pscale, l)
        m_l = _merge(p_l.reshape(B * S, -1), ys_l.reshape(B * S, -1), zl, COL_S5, COL_GATE, a_l.reshape(B * S, -1),
                     dsk, wglu_bf, wbp_bf, wbs_bf, wba_bf, l, min(256, S))
        xl = _resid_mm(m_l, wo_bf, xl, mods3, base_l, 2, lat_tpr, l, tm_l)
        xl = peer(xl, base_l, lat_tpr, l, l == L - 1)

        if need_ctx:
            a_c = _attention(zc3, COL_Q, zc3, COL_K, zc3, COL_V, None, 0, 0, None, lparams, l, lam_init, tq_c)
            ys_c = _from_chunks(yg[:, n_lat * B:], B, CTX)
            p_c = _pool(zc3, pool_w_bf, pscale, l)
            m_c = _merge(p_c.reshape(B * CTX, -1), ys_c.reshape(B * CTX, -1), zc, COL_S5, COL_GATE,
                         a_c.reshape(B * CTX, -1), dsk, wglu_bf, wbp_bf, wbs_bf, wba_bf, l, min(256, CTX))
            xc = _resid_mm(m_c, wo_bf, xc, mods3, base_c, 2, ctx_tpr, l, tm_c)
            xc = peer(xc, base_c, ctx_tpr, l, False)

    return xl.reshape(B, S, D)
```
